```python
import math
import jax, jax.numpy as jnp
from jax import lax
import numpy as np

D_MODEL = 4096
BATCH = 2
SEQ = 8192
DEPTH = 2

MIX_WIDTH = D_MODEL
HEAD_DIM = 128
CONV_CH = MIX_WIDTH // 2
ATTN_WIDTH = MIX_WIDTH - CONV_CH
N_ATTN_HEADS = ATTN_WIDTH // HEAD_DIM
N_CONV_GROUPS = CONV_CH // HEAD_DIM
CONV_KERNEL = 31
DILATED_PATTERNS = ((128, 1), (512, 4), (2048, 16))
D_FF = 4 * D_MODEL
IN_COLS = 2 * CONV_CH + 3 * ATTN_WIDTH
RMS_EPS = 1e-6
LN_EPS = 1e-5
NEG_BIG = -1e30

kernel_name = "hymba_conformer_dilated_alibi_encoder"


def rms_norm(x, g):
    xf = x.astype(jnp.float32)
    y = xf * lax.rsqrt(jnp.mean(xf * xf, axis=-1, keepdims=True) + RMS_EPS)
    return (y * g.astype(jnp.float32)).astype(x.dtype)


def layer_norm(x, g, b):
    xf = x.astype(jnp.float32)
    mu = jnp.mean(xf, axis=-1, keepdims=True)
    xc = xf - mu
    y = xc * lax.rsqrt(jnp.mean(xc * xc, axis=-1, keepdims=True) + LN_EPS)
    return (y * g.astype(jnp.float32) + b.astype(jnp.float32)).astype(x.dtype)


def alibi_slopes(n_heads):
    start = 2.0 ** (-8.0 / n_heads)
    return jnp.asarray(start ** np.arange(1, n_heads + 1), dtype=jnp.float32)


def dilated_band_attention(q, k, v, window, dilation, slopes):
    B, S, H, Dh = q.shape
    d = dilation
    half = window // (2 * d)
    L = S // d
    nb = -(-L // half)
    Lp = nb * half

    def strided(t):
        return t.reshape(B, L, d, H, Dh).transpose(0, 2, 3, 1, 4).reshape(B * d, H, L, Dh)

    qs, ks, vs = strided(q), strided(k), strided(v)
    qb = jnp.pad(qs, ((0, 0), (0, 0), (0, Lp - L), (0, 0))).reshape(B * d, H, nb, half, Dh)

    def context(t):
        tp = jnp.pad(t, ((0, 0), (0, 0), (half, Lp - L + half), (0, 0)))
        tp = tp.reshape(B * d, H, nb + 2, half, Dh)
        return jnp.concatenate([tp[:, :, 0:nb], tp[:, :, 1:nb + 1], tp[:, :, 2:nb + 2]], axis=3)

    kc, vc = context(ks), context(vs)

    t_idx = jnp.arange(half)
    u_idx = jnp.arange(3 * half)
    blk = jnp.arange(nb)
    rel = u_idx[None, :] - half - t_idx[:, None]
    kj = blk[:, None] * half - half + u_idx[None, :]
    valid = (jnp.abs(rel) <= half)[None] & ((kj >= 0) & (kj < L))[:, None, :]
    dist = (jnp.abs(rel) * d).astype(jnp.float32)

    s = jnp.einsum('nhbqd,nhbkd->nhbqk', qb, kc).astype(jnp.float32) * (Dh ** -0.5)
    s = s - slopes[:, None, None, None] * dist
    s = jnp.where(valid, s, NEG_BIG)
    m = jnp.max(s, axis=-1, keepdims=True)
    p = jnp.exp(s - m)
    den = jnp.sum(p, axis=-1)
    o = jnp.einsum('nhbqk,nhbkd->nhbqd', p.astype(v.dtype), vc).astype(jnp.float32) / den[..., None]
    lse = m[..., 0] + jnp.log(den)

    o = o.reshape(B, d, H, Lp, Dh)[:, :, :, :L].transpose(0, 3, 1, 2, 4).reshape(B, S, H, Dh)
    lse = lse.reshape(B, d, H, Lp)[..., :L].transpose(0, 3, 1, 2).reshape(B, S, H)
    return o, lse


def hybrid_mixer(h, w_in, b_glu, w_dw, b_dw, ln_g, ln_b, w_out):
    B, S, _ = h.shape
    proj = h @ w_in

    glu_in = proj[..., :2 * CONV_CH] + b_glu
    a, g = glu_in[..., :CONV_CH], glu_in[..., CONV_CH:]
    c = a * jax.nn.sigmoid(g)
    c = lax.conv_general_dilated(
        c, w_dw[:, None, :].astype(c.dtype), window_strides=(1,),
        padding=((CONV_KERNEL // 2, CONV_KERNEL // 2),),
        dimension_numbers=('NWC', 'WIO', 'NWC'),
        feature_group_count=CONV_CH) + b_dw
    c = jax.nn.silu(layer_norm(c, ln_g, ln_b))

    off = 2 * CONV_CH
    q = proj[..., off:off + ATTN_WIDTH].reshape(B, S, N_ATTN_HEADS, HEAD_DIM)
    k = proj[..., off + ATTN_WIDTH:off + 2 * ATTN_WIDTH].reshape(B, S, N_ATTN_HEADS, HEAD_DIM)
    v = proj[..., off + 2 * ATTN_WIDTH:off + 3 * ATTN_WIDTH].reshape(B, S, N_ATTN_HEADS, HEAD_DIM)
    slopes = alibi_slopes(N_ATTN_HEADS)
    outs, lses = [], []
    for window, dilation in DILATED_PATTERNS:
        o_i, lse_i = dilated_band_attention(q, k, v, window, dilation, slopes)
        outs.append(o_i)
        lses.append(lse_i)
    wts = jax.nn.softmax(jnp.stack(lses, axis=0), axis=0)
    attn = jnp.einsum('pbsh,pbshd->bshd', wts, jnp.stack(outs, axis=0))
    attn = attn.astype(h.dtype).reshape(B, S, ATTN_WIDTH)

    return jnp.concatenate([c, attn], axis=-1) @ w_out


def setup_inputs(seed: int = 0) -> dict:
    key = jax.random.key(seed)
    ks = jax.random.split(key, 14)
    f32 = jnp.float32
    x = jax.random.normal(ks[0], (BATCH, SEQ, D_MODEL), f32)
    norm1_g = 1.0 + 0.02 * jax.random.normal(ks[1], (DEPTH, D_MODEL), f32)
    w_in = jax.random.normal(ks[2], (DEPTH, D_MODEL, IN_COLS), f32) * D_MODEL ** -0.5
    b_glu = 0.02 * jax.random.normal(ks[3], (DEPTH, 2 * CONV_CH), f32)
    w_dw = jax.random.normal(ks[4], (DEPTH, CONV_KERNEL, CONV_CH), f32) * CONV_KERNEL ** -0.5
    b_dw = 0.02 * jax.random.normal(ks[5], (DEPTH, CONV_CH), f32)
    ln_g = 1.0 + 0.02 * jax.random.normal(ks[6], (DEPTH, CONV_CH), f32)
    ln_b = 0.02 * jax.random.normal(ks[7], (DEPTH, CONV_CH), f32)
    w_out = jax.random.normal(ks[8], (DEPTH, MIX_WIDTH, D_MODEL), f32) * MIX_WIDTH ** -0.5
    norm2_g = 1.0 + 0.02 * jax.random.normal(ks[9], (DEPTH, D_MODEL), f32)
    w_up = jax.random.normal(ks[10], (DEPTH, D_MODEL, D_FF), f32) * D_MODEL ** -0.5
    w_down = jax.random.normal(ks[11], (DEPTH, D_FF, D_MODEL), f32) * D_FF ** -0.5
    final_g = 1.0 + 0.02 * jax.random.normal(ks[12], (D_MODEL,), f32)
    return {"x": x, "norm1_g": norm1_g, "w_in": w_in, "b_glu": b_glu, "w_dw": w_dw,
            "b_dw": b_dw, "ln_g": ln_g, "ln_b": ln_b, "w_out": w_out, "norm2_g": norm2_g,
            "w_up": w_up, "w_down": w_down, "final_g": final_g}


def reference(x, norm1_g, w_in, b_glu, w_dw, b_dw, ln_g, ln_b, w_out, norm2_g, w_up, w_down, final_g):
    for l in range(DEPTH):
        h = rms_norm(x, norm1_g[l])
        x = x + hybrid_mixer(h, w_in[l], b_glu[l], w_dw[l], b_dw[l], ln_g[l], ln_b[l], w_out[l])
        hid = jnp.square(jax.nn.relu(rms_norm(x, norm2_g[l]) @ w_up[l]))
        x = x + hid @ w_down[l]
    return rms_norm(x, final_g)
```

```python
import functools

import jax
import jax.numpy as jnp
import numpy as np
from jax import lax
from jax.experimental import pallas as pl
from jax.experimental.pallas import tpu as pltpu

D_MODEL = 4096
HEAD_DIM = 128
CONV_CH = 2048
ATTN_WIDTH = 2048
N_HEADS = ATTN_WIDTH // HEAD_DIM
CONV_KERNEL = 31
CONV_HALF = CONV_KERNEL // 2
DILATED_PATTERNS = ((128, 1), (512, 4), (2048, 16))
D_FF = 4 * D_MODEL
GLU_COLS = 2 * CONV_CH
QKV_COLS = 3 * ATTN_WIDTH
RMS_EPS = 1e-6
LN_EPS = 1e-5
MASK_DIST = 1e30
BAND_HALF = 64

V7X_VMEM_LIMIT_BYTES = 56 * 1024 * 1024

_F32 = jnp.float32
_BF16 = jnp.bfloat16

_SLOPES = [float(s) for s in np.asarray((2.0 ** (-8.0 / N_HEADS)) ** np.arange(1, N_HEADS + 1), dtype=np.float32)]


def _params(semantics):
    return pltpu.CompilerParams(dimension_semantics=semantics, vmem_limit_bytes=V7X_VMEM_LIMIT_BYTES)


def _rmsnorm_kernel(x_ref, g_ref, o_ref):
    x = x_ref[...]
    ms = jnp.mean(x * x, axis=-1, keepdims=True)
    o_ref[...] = (x * lax.rsqrt(ms + RMS_EPS) * g_ref[...]).astype(o_ref.dtype)


def _rmsnorm(x, g, out_dtype, *, tm=256):
    m, dm = x.shape
    return pl.pallas_call(
        _rmsnorm_kernel,
        out_shape=jax.ShapeDtypeStruct((m, dm), out_dtype),
        grid=(m // tm,),
        in_specs=[pl.BlockSpec((tm, dm), lambda i: (i, 0)), pl.BlockSpec((1, dm), lambda i: (0, 0))],
        out_specs=pl.BlockSpec((tm, dm), lambda i: (i, 0)),
        compiler_params=_params(("parallel",)),
        name="rmsnorm",
    )(x, g.reshape(1, dm))


def _mm_bias_kernel(a_ref, w_ref, b_ref, o_ref):
    acc = jnp.dot(a_ref[...], w_ref[...], preferred_element_type=_F32)
    o_ref[...] = (acc + b_ref[...]).astype(o_ref.dtype)


def _mm_plain_kernel(a_ref, w_ref, o_ref):
    o_ref[...] = jnp.dot(a_ref[...], w_ref[...], preferred_element_type=_F32).astype(o_ref.dtype)


def _mm_relu2_kernel(a_ref, w_ref, o_ref):
    acc = jnp.dot(a_ref[...], w_ref[...], preferred_element_type=_F32)
    o_ref[...] = jnp.square(jnp.maximum(acc, 0.0)).astype(o_ref.dtype)


def _mm_two_res_kernel(a0_ref, a1_ref, w_ref, r_ref, o_ref):
    k0 = a0_ref.shape[1]
    acc = jnp.dot(a0_ref[...], w_ref[:k0, :], preferred_element_type=_F32)
    acc += jnp.dot(a1_ref[...], w_ref[k0:, :], preferred_element_type=_F32)
    o_ref[...] = acc + r_ref[...]


def _mm_ktiled_res_kernel(a_ref, w_ref, r_ref, o_ref, acc_ref):
    k = pl.program_id(2)

    @pl.when(k == 0)
    def _():
        acc_ref[...] = r_ref[...]

    acc_ref[...] += jnp.dot(a_ref[...], w_ref[...], preferred_element_type=_F32)

    @pl.when(k == pl.num_programs(2) - 1)
    def _():
        o_ref[...] = acc_ref[...]


def _matmul(kernel, a, w, extra=(), *, n_out, n_off=0, out_dtype, bm=1024, bn=1024, name):
    m, k = a.shape
    in_specs = [pl.BlockSpec((bm, k), lambda i, j: (i, 0)),
                pl.BlockSpec((k, bn), lambda i, j: (0, j + n_off // bn))]
    for e in extra:
        if e.shape[0] == 1:
            in_specs.append(pl.BlockSpec((1, bn), lambda i, j: (0, j)))
        else:
            in_specs.append(pl.BlockSpec((bm, bn), lambda i, j: (i, j)))
    return pl.pallas_call(
        kernel,
        out_shape=jax.ShapeDtypeStruct((m, n_out), out_dtype),
        grid=(m // bm, n_out // bn),
        in_specs=in_specs,
        out_specs=pl.BlockSpec((bm, bn), lambda i, j: (i, j)),
        compiler_params=_params(("parallel", "parallel")),
        name=name,
    )(a, w, *extra)


def _out_proj(c, attn, w, res, *, bm=1024, bn=1024):
    m, kc = c.shape
    ka = attn.shape[1]
    n = w.shape[1]
    return pl.pallas_call(
        _mm_two_res_kernel,
        out_shape=jax.ShapeDtypeStruct((m, n), _F32),
        grid=(m // bm, n // bn),
        in_specs=[pl.BlockSpec((bm, kc), lambda i, j: (i, 0)),
                  pl.BlockSpec((bm, ka), lambda i, j: (i, 0)),
                  pl.BlockSpec((kc + ka, bn), lambda i, j: (0, j)),
                  pl.BlockSpec((bm, bn), lambda i, j: (i, j))],
        out_specs=pl.BlockSpec((bm, bn), lambda i, j: (i, j)),
        compiler_params=_params(("parallel", "parallel")),
        name="out_proj",
    )(c, attn, w, res)


def _down_proj(a, w, res, *, bm=1024, bn=1024, bk=2048):
    m, k = a.shape
    n = w.shape[1]
    return pl.pallas_call(
        _mm_ktiled_res_kernel,
        out_shape=jax.ShapeDtypeStruct((m, n), _F32),
        grid=(m // bm, n // bn, k // bk),
        in_specs=[pl.BlockSpec((bm, bk), lambda i, j, kk: (i, kk)),
                  pl.BlockSpec((bk, bn), lambda i, j, kk: (kk, j)),
                  pl.BlockSpec((bm, bn), lambda i, j, kk: (i, j))],
        out_specs=pl.BlockSpec((bm, bn), lambda i, j, kk: (i, j)),
        scratch_shapes=[pltpu.VMEM((bm, bn), _F32)],
        compiler_params=_params(("parallel", "parallel", "arbitrary")),
        name="down_proj",
    )(a, w, res)


CONV_TQ = 512
CONV_HALO = 16
CONV_RB = 128
CONV_LN_ROWS = 16
LANES = 128


def _conv_branch_kernel(ap_ref, ac_ref, an_ref, gp_ref, gc_ref, gn_ref, w_ref, b_ref, lg_ref, lb_ref,
                        o_ref, glu_ref, y_ref):
    i = pl.program_id(1)
    last = pl.num_programs(1) - 1
    tq = CONV_TQ
    nslab = CONV_CH // LANES

    def glu(a, g):
        return a * jax.nn.sigmoid(g)

    for sl in range(nslab):
        cs = slice(sl * LANES, (sl + 1) * LANES)
        glu_ref[sl, 0:CONV_HALO, :] = jnp.where(i > 0, glu(ap_ref[0, :, cs], gp_ref[0, :, cs]), 0.0)
        glu_ref[sl, CONV_HALO:CONV_HALO + tq, :] = glu(ac_ref[0, :, cs], gc_ref[0, :, cs])
        glu_ref[sl, CONV_HALO + tq:, :] = jnp.where(i < last, glu(an_ref[0, :, cs], gn_ref[0, :, cs]), 0.0)

    half_rows = CONV_RB // 2

    def block(rb, carry):
        r0 = pl.multiple_of(rb * CONV_RB, CONV_RB)
        for sl in range(nslab):
            cs = slice(sl * LANES, (sl + 1) * LANES)
            acc0 = jnp.broadcast_to(b_ref[:, cs], (half_rows, LANES))
            acc1 = acc0
            for t in range(CONV_KERNEL):
                w = w_ref[t:t + 1, cs]
                st = r0 + t + (CONV_HALO - CONV_HALF)
                acc0 = acc0 + glu_ref[sl, pl.ds(st, half_rows, stride=2), :] * w
                acc1 = acc1 + glu_ref[sl, pl.ds(st + 1, half_rows, stride=2), :] * w
            y_ref[sl, pl.ds(r0, half_rows, stride=2), :] = acc0
            y_ref[sl, pl.ds(r0 + 1, half_rows, stride=2), :] = acc1
        for c0 in range(0, CONV_RB, CONV_LN_ROWS):
            rows = pl.ds(r0 + c0, CONV_LN_ROWS)
            y = jnp.concatenate([y_ref[sl, rows, :] for sl in range(nslab)], axis=-1)
            mu = jnp.mean(y, axis=-1, keepdims=True)
            yc = y - mu
            var = jnp.mean(yc * yc, axis=-1, keepdims=True)
            z = yc * lax.rsqrt(var + LN_EPS) * lg_ref[...] + lb_ref[...]
            o_ref[0, rows, :] = (z * jax.nn.sigmoid(z)).astype(o_ref.dtype)
        return carry

    lax.fori_loop(0, tq // CONV_RB, block, 0)


def _conv_branch(glu_in, w_dw, b_dw, ln_g, ln_b):
    b, s, _ = glu_in.shape
    tq, halo = CONV_TQ, CONV_HALO
    nh = tq // halo
    last_halo = s // halo - 1
    c = CONV_CH

    def cur(col):
        return pl.BlockSpec((1, tq, c), lambda bi, i: (bi, i, col))

    def prev(col):
        return pl.BlockSpec((1, halo, c), lambda bi, i: (bi, jnp.maximum(i * nh - 1, 0), col))

    def nxt(col):
        return pl.BlockSpec((1, halo, c), lambda bi, i: (bi, jnp.minimum((i + 1) * nh, last_halo), col))

    def row(n):
        return pl.BlockSpec((n, c), lambda bi, i: (0, 0))

    return pl.pallas_call(
        _conv_branch_kernel,
        out_shape=jax.ShapeDtypeStruct((b, s, c), _BF16),
        grid=(b, s // tq),
        in_specs=[prev(0), cur(0), nxt(0), prev(1), cur(1), nxt(1), row(CONV_KERNEL), row(1), row(1), row(1)],
        out_specs=pl.BlockSpec((1, tq, c), lambda bi, i: (bi, i, 0)),
        scratch_shapes=[pltpu.VMEM((c // LANES, tq + 2 * halo, LANES), _F32),
                        pltpu.VMEM((c // LANES, tq, LANES), _F32)],
        compiler_params=_params(("parallel", "parallel")),
        name="conv_branch",
    )(glu_in, glu_in, glu_in, glu_in, glu_in, glu_in, w_dw, b_dw.reshape(1, c), ln_g.reshape(1, c),
      ln_b.reshape(1, c))


ATT_TQ = 512
ATT_SB = 128


def _attn_kernel(q_ref, kp_ref, kc_ref, kn_ref, vp_ref, vc_ref, vn_ref, o_ref, lse_ref, dist_ref,
                 *, dilation, seq_len):
    i = pl.program_id(2)
    tq, sb, half = ATT_TQ, ATT_SB, BAND_HALF
    kw = sb + 2 * half
    nsub = tq // sb
    scale = HEAD_DIM ** -0.5

    row = lax.broadcasted_iota(jnp.int32, (sb, kw), 0)
    col = lax.broadcasted_iota(jnp.int32, (sb, kw), 1)
    rel = col - half - row
    dist = (jnp.abs(rel) * dilation).astype(_F32)
    for j in range(nsub):
        kidx = i * tq + (j * sb - half) + col
        valid = (jnp.abs(rel) <= half) & (kidx >= 0) & (kidx < seq_len)
        dist_ref[j] = jnp.where(valid, dist, MASK_DIST)

    for j in range(nsub):
        rows = slice(j * sb, (j + 1) * sb)
        lo = j * sb - half
        for h in range(N_HEADS):
            hs = slice(h * HEAD_DIM, (h + 1) * HEAD_DIM)

            def window(prev_ref, cur_ref, next_ref):
                parts = []
                if lo < 0:
                    parts.append(prev_ref[0, half + lo:, hs])
                parts.append(cur_ref[0, max(lo, 0):min(lo + kw, tq), hs])
                if lo + kw > tq:
                    parts.append(next_ref[0, :lo + kw - tq, hs])
                return parts[0] if len(parts) == 1 else jnp.concatenate(parts, axis=0)

            q = q_ref[0, rows, hs]
            s = lax.dot_general(q, window(kp_ref, kc_ref, kn_ref), (((1,), (1,)), ((), ())),
                                preferred_element_type=_F32)
            s = s * scale - _SLOPES[h] * dist_ref[j]
            m = jnp.max(s, axis=-1, keepdims=True)
            p = jnp.exp(s - m)
            den = jnp.sum(p, axis=-1, keepdims=True)
            o = jnp.dot(p.astype(_BF16), window(vp_ref, vc_ref, vn_ref), preferred_element_type=_F32)
            o_ref[0, rows, hs] = o / den
            lse_ref[0, rows, hs] = jnp.broadcast_to(m + jnp.log(den), (sb, HEAD_DIM))


def _dilated_attention(qkv, dilation):
    b, s, _ = qkv.shape
    d = dilation
    l = s // d
    tq, half, w = ATT_TQ, BAND_HALF, ATTN_WIDTH
    nh = tq // half
    last_halo = l // half - 1
    view = qkv.reshape(b, l, d * QKV_COLS)

    def cur(part):
        return pl.BlockSpec((1, tq, w), lambda bi, r, i: (bi, i, 3 * r + part))

    def prev(part):
        return pl.BlockSpec((1, half, w), lambda bi, r, i: (bi, jnp.maximum(i * nh - 1, 0), 3 * r + part))

    def nxt(part):
        return pl.BlockSpec((1, half, w),
                            lambda bi, r, i: (bi, jnp.minimum((i + 1) * nh, last_halo), 3 * r + part))

    out_spec = pl.BlockSpec((1, tq, w), lambda bi, r, i: (bi, i, r))
    o, lse = pl.pallas_call(
        functools.partial(_attn_kernel, dilation=d, seq_len=l),
        out_shape=[jax.ShapeDtypeStruct((b, l, d * w), _F32), jax.ShapeDtypeStruct((b, l, d * w), _F32)],
        grid=(b, d, l // tq),
        in_specs=[cur(0), prev(1), cur(1), nxt(1), prev(2), cur(2), nxt(2)],
        out_specs=[out_spec, out_spec],
        scratch_shapes=[pltpu.VMEM((tq // ATT_SB, ATT_SB, ATT_SB + 2 * half), _F32)],
        compiler_params=_params(("parallel", "parallel", "parallel")),
        name=f"dilated_attn_d{d}",
    )(view, view, view, view, view, view, view)
    return o.reshape(b, s, w), lse.reshape(b, s, w)


def _combine_kernel(o0_ref, o1_ref, o2_ref, l0_ref, l1_ref, l2_ref, out_ref):
    l0, l1, l2 = l0_ref[...], l1_ref[...], l2_ref[...]
    mx = jnp.maximum(jnp.maximum(l0, l1), l2)
    e0, e1, e2 = jnp.exp(l0 - mx), jnp.exp(l1 - mx), jnp.exp(l2 - mx)
    num = e0 * o0_ref[...] + e1 * o1_ref[...] + e2 * o2_ref[...]
    out_ref[...] = (num / (e0 + e1 + e2)).astype(out_ref.dtype)


def _combine(outs, lses, *, tm=256):
    m, w = outs[0].shape
    spec = pl.BlockSpec((tm, w), lambda i: (i, 0))
    return pl.pallas_call(
        _combine_kernel,
        out_shape=jax.ShapeDtypeStruct((m, w), _BF16),
        grid=(m // tm,),
        in_specs=[spec] * 6,
        out_specs=spec,
        compiler_params=_params(("parallel",)),
        name="attn_combine",
    )(*outs, *lses)


def kernel(x, norm1_g, w_in, b_glu, w_dw, b_dw, ln_g, ln_b, w_out, norm2_g, w_up, w_down, final_g):
    b, s, dm = x.shape
    m = b * s
    depth = w_in.shape[0]
    x = x.reshape(m, dm)
    for l in range(depth):
        w_in_l = w_in[l].astype(_BF16)
        w_out_l = w_out[l].astype(_BF16)
        w_up_l = w_up[l].astype(_BF16)
        w_down_l = w_down[l].astype(_BF16)

        h = _rmsnorm(x, norm1_g[l], _BF16)
        glu_in = _matmul(_mm_bias_kernel, h, w_in_l, (b_glu[l].reshape(1, GLU_COLS),), n_out=GLU_COLS,
                         out_dtype=_F32, name="in_proj_glu")
        qkv = _matmul(_mm_plain_kernel, h, w_in_l, n_out=QKV_COLS, n_off=GLU_COLS, out_dtype=_BF16,
                      name="in_proj_qkv")

        c = _conv_branch(glu_in.reshape(b, s, GLU_COLS), w_dw[l], b_dw[l], ln_g[l], ln_b[l])

        qkv = qkv.reshape(b, s, QKV_COLS)
        outs, lses = [], []
        for _, dilation in DILATED_PATTERNS:
            o, lse = _dilated_attention(qkv, dilation)
            outs.append(o.reshape(m, ATTN_WIDTH))
            lses.append(lse.reshape(m, ATTN_WIDTH))
        attn = _combine(outs, lses)

        x = _out_proj(c.reshape(m, CONV_CH), attn, w_out_l, x)
        h2 = _rmsnorm(x, norm2_g[l], _BF16)
        hid = _matmul(_mm_relu2_kernel, h2, w_up_l, n_out=D_FF, out_dtype=_BF16, name="mlp_up")
        x = _down_proj(hid, w_down_l, x)
    return _rmsnorm(x, final_g, _F32).reshape(b, s, dm)
```

```python
import functools

import jax
import jax.numpy as jnp
import numpy as np
from jax import lax
from jax.experimental import pallas as pl
from jax.experimental.pallas import tpu as pltpu

D_MODEL = 4096
HEAD_DIM = 128
CONV_CH = 2048
ATTN_WIDTH = 2048
N_HEADS = ATTN_WIDTH // HEAD_DIM
CONV_KERNEL = 31
CONV_HALF = CONV_KERNEL // 2
DILATED_PATTERNS = ((128, 1), (512, 4), (2048, 16))
D_FF = 4 * D_MODEL
GLU_COLS = 2 * CONV_CH
QKV_COLS = 3 * ATTN_WIDTH
RMS_EPS = 1e-6
LN_EPS = 1e-5
MASK_DIST = 1e30
BAND_HALF = 64

V7X_VMEM_LIMIT_BYTES = 56 * 1024 * 1024

_F32 = jnp.float32
_BF16 = jnp.bfloat16

_SLOPES = [float(s) for s in np.asarray((2.0 ** (-8.0 / N_HEADS)) ** np.arange(1, N_HEADS + 1), dtype=np.float32)]


def _params(semantics):
    return pltpu.CompilerParams(dimension_semantics=semantics, vmem_limit_bytes=V7X_VMEM_LIMIT_BYTES)


def _rmsnorm_kernel(x_ref, g_ref, o_ref):
    x = x_ref[...]
    ms = jnp.mean(x * x, axis=-1, keepdims=True)
    o_ref[...] = (x * lax.rsqrt(ms + RMS_EPS) * g_ref[...]).astype(o_ref.dtype)


def _rmsnorm(x, g, out_dtype, *, tm=256):
    m, dm = x.shape
    return pl.pallas_call(
        _rmsnorm_kernel,
        out_shape=jax.ShapeDtypeStruct((m, dm), out_dtype),
        grid=(m // tm,),
        in_specs=[pl.BlockSpec((tm, dm), lambda i: (i, 0)), pl.BlockSpec((1, dm), lambda i: (0, 0))],
        out_specs=pl.BlockSpec((tm, dm), lambda i: (i, 0)),
        compiler_params=_params(("parallel",)),
        name="rmsnorm",
    )(x, g.reshape(1, dm))


def _cast_kernel(w_ref, o_ref):
    o_ref[...] = w_ref[...].astype(o_ref.dtype)


def _cast_layer(w, layer, *, bk=512, bn=2048):
    _, k, n = w.shape
    return pl.pallas_call(
        _cast_kernel,
        out_shape=jax.ShapeDtypeStruct((k, n), _BF16),
        grid=(k // bk, n // bn),
        in_specs=[pl.BlockSpec((None, bk, bn), lambda i, j: (layer, i, j))],
        out_specs=pl.BlockSpec((bk, bn), lambda i, j: (i, j)),
        compiler_params=_params(("parallel", "parallel")),
        name="weight_cast",
    )(w)


def _mm_bias_kernel(a_ref, w_ref, b_ref, o_ref):
    acc = jnp.dot(a_ref[...], w_ref[...], preferred_element_type=_F32)
    o_ref[...] = (acc + b_ref[...]).astype(o_ref.dtype)


def _mm_plain_kernel(a_ref, w_ref, o_ref):
    o_ref[...] = jnp.dot(a_ref[...], w_ref[...], preferred_element_type=_F32).astype(o_ref.dtype)


def _mm_relu2_kernel(a_ref, w_ref, o_ref):
    acc = jnp.dot(a_ref[...], w_ref[...], preferred_element_type=_F32)
    o_ref[...] = jnp.square(jnp.maximum(acc, 0.0)).astype(o_ref.dtype)


def _mm_two_res_kernel(a0_ref, a1_ref, w_ref, r_ref, o_ref):
    k0 = a0_ref.shape[1]
    acc = jnp.dot(a0_ref[...], w_ref[:k0, :], preferred_element_type=_F32)
    acc += jnp.dot(a1_ref[...], w_ref[k0:, :], preferred_element_type=_F32)
    o_ref[...] = acc + r_ref[...]


def _mm_ktiled_res_kernel(a_ref, w_ref, r_ref, o_ref):
    base = jnp.where(pl.program_id(2) == 0, r_ref[...], o_ref[...])
    o_ref[...] = base + jnp.dot(a_ref[...], w_ref[...], preferred_element_type=_F32)


def _matmul(kernel, a, w, extra=(), *, n_out, n_off=0, out_dtype, bm=1024, bn=1024, name):
    m, k = a.shape
    in_specs = [pl.BlockSpec((bm, k), lambda i, j: (i, 0)),
                pl.BlockSpec((k, bn), lambda i, j: (0, j + n_off // bn))]
    for e in extra:
        if e.shape[0] == 1:
            in_specs.append(pl.BlockSpec((1, bn), lambda i, j: (0, j)))
        else:
            in_specs.append(pl.BlockSpec((bm, bn), lambda i, j: (i, j)))
    return pl.pallas_call(
        kernel,
        out_shape=jax.ShapeDtypeStruct((m, n_out), out_dtype),
        grid=(m // bm, n_out // bn),
        in_specs=in_specs,
        out_specs=pl.BlockSpec((bm, bn), lambda i, j: (i, j)),
        compiler_params=_params(("parallel", "parallel")),
        name=name,
    )(a, w, *extra)


def _out_proj(c, attn, w, res, *, bm=1024, bn=1024):
    m, kc = c.shape
    ka = attn.shape[1]
    n = w.shape[1]
    return pl.pallas_call(
        _mm_two_res_kernel,
        out_shape=jax.ShapeDtypeStruct((m, n), _F32),
        grid=(m // bm, n // bn),
        in_specs=[pl.BlockSpec((bm, kc), lambda i, j: (i, 0)),
                  pl.BlockSpec((bm, ka), lambda i, j: (i, 0)),
                  pl.BlockSpec((kc + ka, bn), lambda i, j: (0, j)),
                  pl.BlockSpec((bm, bn), lambda i, j: (i, j))],
        out_specs=pl.BlockSpec((bm, bn), lambda i, j: (i, j)),
        compiler_params=_params(("parallel", "parallel")),
        name="out_proj",
    )(c, attn, w, res)


def _down_proj(a, w, res, *, bm=1024, bn=1024, bk=2048):
    m, k = a.shape
    n = w.shape[1]
    return pl.pallas_call(
        _mm_ktiled_res_kernel,
        out_shape=jax.ShapeDtypeStruct((m, n), _F32),
        grid=(m // bm, n // bn, k // bk),
        in_specs=[pl.BlockSpec((bm, bk), lambda i, j, kk: (i, kk)),
                  pl.BlockSpec((bk, bn), lambda i, j, kk: (kk, j)),
                  pl.BlockSpec((bm, bn), lambda i, j, kk: (i, j))],
        out_specs=pl.BlockSpec((bm, bn), lambda i, j, kk: (i, j)),
        compiler_params=_params(("parallel", "parallel", "arbitrary")),
        name="down_proj",
    )(a, w, res)


CONV_TQ = 512
CONV_HALO = 16
CONV_RB = 128
CONV_LN_ROWS = 16
LANES = 128


def _conv_branch_kernel(ap_ref, ac_ref, an_ref, gp_ref, gc_ref, gn_ref, w_ref, b_ref, lg_ref, lb_ref,
                        o_ref, glu_ref, y_ref):
    i = pl.program_id(1)
    last = pl.num_programs(1) - 1
    tq = CONV_TQ
    nslab = CONV_CH // LANES

    def glu(a, g):
        return a * jax.nn.sigmoid(g)

    for sl in range(nslab):
        cs = slice(sl * LANES, (sl + 1) * LANES)
        glu_ref[sl, 0:CONV_HALO, :] = jnp.where(i > 0, glu(ap_ref[0, :, cs], gp_ref[0, :, cs]), 0.0)
        glu_ref[sl, CONV_HALO:CONV_HALO + tq, :] = glu(ac_ref[0, :, cs], gc_ref[0, :, cs])
        glu_ref[sl, CONV_HALO + tq:, :] = jnp.where(i < last, glu(an_ref[0, :, cs], gn_ref[0, :, cs]), 0.0)

    half_rows = CONV_RB // 2

    def block(rb, carry):
        r0 = pl.multiple_of(rb * CONV_RB, CONV_RB)
        for sl in range(nslab):
            cs = slice(sl * LANES, (sl + 1) * LANES)
            acc0 = jnp.broadcast_to(b_ref[:, cs], (half_rows, LANES))
            acc1 = acc0
            for t in range(CONV_KERNEL):
                w = w_ref[t:t + 1, cs]
                st = r0 + t + (CONV_HALO - CONV_HALF)
                acc0 = acc0 + glu_ref[sl, pl.ds(st, half_rows, stride=2), :] * w
                acc1 = acc1 + glu_ref[sl, pl.ds(st + 1, half_rows, stride=2), :] * w
            y_ref[sl, pl.ds(r0, half_rows, stride=2), :] = acc0
            y_ref[sl, pl.ds(r0 + 1, half_rows, stride=2), :] = acc1
        for c0 in range(0, CONV_RB, CONV_LN_ROWS):
            rows = pl.ds(r0 + c0, CONV_LN_ROWS)
            y = jnp.concatenate([y_ref[sl, rows, :] for sl in range(nslab)], axis=-1)
            mu = jnp.mean(y, axis=-1, keepdims=True)
            yc = y - mu
            var = jnp.mean(yc * yc, axis=-1, keepdims=True)
            z = yc * lax.rsqrt(var + LN_EPS) * lg_ref[...] + lb_ref[...]
            o_ref[0, rows, :] = (z * jax.nn.sigmoid(z)).astype(o_ref.dtype)
        return carry

    lax.fori_loop(0, tq // CONV_RB, block, 0)


def _conv_branch(glu_in, w_dw, b_dw, ln_g, ln_b):
    b, s, _ = glu_in.shape
    tq, halo = CONV_TQ, CONV_HALO
    nh = tq // halo
    last_halo = s // halo - 1
    c = CONV_CH

    def cur(col):
        return pl.BlockSpec((1, tq, c), lambda bi, i: (bi, i, col))

    def prev(col):
        return pl.BlockSpec((1, halo, c), lambda bi, i: (bi, jnp.maximum(i * nh - 1, 0), col))

    def nxt(col):
        return pl.BlockSpec((1, halo, c), lambda bi, i: (bi, jnp.minimum((i + 1) * nh, last_halo), col))

    def row(n):
        return pl.BlockSpec((n, c), lambda bi, i: (0, 0))

    return pl.pallas_call(
        _conv_branch_kernel,
        out_shape=jax.ShapeDtypeStruct((b, s, c), _BF16),
        grid=(b, s // tq),
        in_specs=[prev(0), cur(0), nxt(0), prev(1), cur(1), nxt(1), row(CONV_KERNEL), row(1), row(1), row(1)],
        out_specs=pl.BlockSpec((1, tq, c), lambda bi, i: (bi, i, 0)),
        scratch_shapes=[pltpu.VMEM((c // LANES, tq + 2 * halo, LANES), _F32),
                        pltpu.VMEM((c // LANES, tq, LANES), _F32)],
        compiler_params=_params(("parallel", "parallel")),
        name="conv_branch",
    )(glu_in, glu_in, glu_in, glu_in, glu_in, glu_in, w_dw, b_dw.reshape(1, c), ln_g.reshape(1, c),
      ln_b.reshape(1, c))


ATT_TQ = 512
ATT_SB = 128


def _attn_kernel(q_ref, kp_ref, kc_ref, kn_ref, vp_ref, vc_ref, vn_ref, o_ref, lse_ref, dist_ref,
                 *, dilation, seq_len):
    i = pl.program_id(2)
    tq, sb, half = ATT_TQ, ATT_SB, BAND_HALF
    kw = sb + 2 * half
    nsub = tq // sb
    scale = HEAD_DIM ** -0.5

    row = lax.broadcasted_iota(jnp.int32, (sb, kw), 0)
    col = lax.broadcasted_iota(jnp.int32, (sb, kw), 1)
    rel = col - half - row
    dist = (jnp.abs(rel) * dilation).astype(_F32)
    for j in range(nsub):
        kidx = i * tq + (j * sb - half) + col
        valid = (jnp.abs(rel) <= half) & (kidx >= 0) & (kidx < seq_len)
        dist_ref[j] = jnp.where(valid, dist, MASK_DIST)

    for j in range(nsub):
        rows = slice(j * sb, (j + 1) * sb)
        lo = j * sb - half
        for h in range(N_HEADS):
            hs = slice(h * HEAD_DIM, (h + 1) * HEAD_DIM)

            def window(prev_ref, cur_ref, next_ref):
                parts = []
                if lo < 0:
                    parts.append(prev_ref[0, half + lo:, hs])
                parts.append(cur_ref[0, max(lo, 0):min(lo + kw, tq), hs])
                if lo + kw > tq:
                    parts.append(next_ref[0, :lo + kw - tq, hs])
                return parts[0] if len(parts) == 1 else jnp.concatenate(parts, axis=0)

            q = q_ref[0, rows, hs]
            s = lax.dot_general(q, window(kp_ref, kc_ref, kn_ref), (((1,), (1,)), ((), ())),
                                preferred_element_type=_F32)
            s = s * scale - _SLOPES[h] * dist_ref[j]
            m = jnp.max(s, axis=-1, keepdims=True)
            p = jnp.exp(s - m)
            den = jnp.sum(p, axis=-1, keepdims=True)
            o = jnp.dot(p.astype(_BF16), window(vp_ref, vc_ref, vn_ref), preferred_element_type=_F32)
            o_ref[0, rows, hs] = o / den
            lse_ref[0, rows, hs] = jnp.broadcast_to(m + jnp.log(den), (sb, HEAD_DIM))


def _dilated_attention(qkv, dilation):
    b, s, _ = qkv.shape
    d = dilation
    l = s // d
    tq, half, w = ATT_TQ, BAND_HALF, ATTN_WIDTH
    nh = tq // half
    last_halo = l // half - 1
    view = qkv.reshape(b, l, d * QKV_COLS)

    def cur(part):
        return pl.BlockSpec((1, tq, w), lambda bi, r, i: (bi, i, 3 * r + part))

    def prev(part):
        return pl.BlockSpec((1, half, w), lambda bi, r, i: (bi, jnp.maximum(i * nh - 1, 0), 3 * r + part))

    def nxt(part):
        return pl.BlockSpec((1, half, w),
                            lambda bi, r, i: (bi, jnp.minimum((i + 1) * nh, last_halo), 3 * r + part))

    out_spec = pl.BlockSpec((1, tq, w), lambda bi, r, i: (bi, i, r))
    o, lse = pl.pallas_call(
        functools.partial(_attn_kernel, dilation=d, seq_len=l),
        out_shape=[jax.ShapeDtypeStruct((b, l, d * w), _F32), jax.ShapeDtypeStruct((b, l, d * w), _F32)],
        grid=(b, d, l // tq),
        in_specs=[cur(0), prev(1), cur(1), nxt(1), prev(2), cur(2), nxt(2)],
        out_specs=[out_spec, out_spec],
        scratch_shapes=[pltpu.VMEM((tq // ATT_SB, ATT_SB, ATT_SB + 2 * half), _F32)],
        compiler_params=_params(("parallel", "parallel", "parallel")),
        name=f"dilated_attn_d{d}",
    )(view, view, view, view, view, view, view)
    return o.reshape(b, s, w), lse.reshape(b, s, w)


def _combine_kernel(o0_ref, o1_ref, o2_ref, l0_ref, l1_ref, l2_ref, out_ref):
    l0, l1, l2 = l0_ref[...], l1_ref[...], l2_ref[...]
    mx = jnp.maximum(jnp.maximum(l0, l1), l2)
    e0, e1, e2 = jnp.exp(l0 - mx), jnp.exp(l1 - mx), jnp.exp(l2 - mx)
    num = e0 * o0_ref[...] + e1 * o1_ref[...] + e2 * o2_ref[...]
    out_ref[...] = (num / (e0 + e1 + e2)).astype(out_ref.dtype)


def _combine(outs, lses, *, tm=256):
    m, w = outs[0].shape
    spec = pl.BlockSpec((tm, w), lambda i: (i, 0))
    return pl.pallas_call(
        _combine_kernel,
        out_shape=jax.ShapeDtypeStruct((m, w), _BF16),
        grid=(m // tm,),
        in_specs=[spec] * 6,
        out_specs=spec,
        compiler_params=_params(("parallel",)),
        name="attn_combine",
    )(*outs, *lses)


def kernel(x, norm1_g, w_in, b_glu, w_dw, b_dw, ln_g, ln_b, w_out, norm2_g, w_up, w_down, final_g):
    b, s, dm = x.shape
    m = b * s
    depth = w_in.shape[0]
    x = x.reshape(m, dm)
    for l in range(depth):
        w_in_l = _cast_layer(w_in, l)
        w_out_l = _cast_layer(w_out, l)
        w_up_l = _cast_layer(w_up, l)
        w_down_l = _cast_layer(w_down, l)

        h = _rmsnorm(x, norm1_g[l], _BF16)
        glu_in = _matmul(_mm_bias_kernel, h, w_in_l, (b_glu[l].reshape(1, GLU_COLS),), n_out=GLU_COLS,
                         out_dtype=_F32, name="in_proj_glu")
        qkv = _matmul(_mm_plain_kernel, h, w_in_l, n_out=QKV_COLS, n_off=GLU_COLS, out_dtype=_BF16,
                      name="in_proj_qkv")

        c = _conv_branch(glu_in.reshape(b, s, GLU_COLS), w_dw[l], b_dw[l], ln_g[l], ln_b[l])

        qkv = qkv.reshape(b, s, QKV_COLS)
        outs, lses = [], []
        for _, dilation in DILATED_PATTERNS:
            o, lse = _dilated_attention(qkv, dilation)
            outs.append(o.reshape(m, ATTN_WIDTH))
            lses.append(lse.reshape(m, ATTN_WIDTH))
        attn = _combine(outs, lses)

        x = _out_proj(c.reshape(m, CONV_CH), attn, w_out_l, x)
        h2 = _rmsnorm(x, norm2_g[l], _BF16)
        hid = _matmul(_mm_relu2_kernel, h2, w_up_l, n_out=D_FF, out_dtype=_BF16, name="mlp_up")
        x = _down_proj(hid, w_down_l, x)
    return _rmsnorm(x, final_g, _F32).reshape(b, s, dm)
```

```python
import jax
import jax.numpy as jnp
import numpy as np
from jax import lax
from jax.experimental import pallas as pl
from jax.experimental.pallas import tpu as pltpu

D_MODEL = 4096
HEAD_DIM = 128
CONV_CH = 2048
ATTN_WIDTH = 2048
N_HEADS = ATTN_WIDTH // HEAD_DIM
CONV_KERNEL = 31
CONV_HALF = CONV_KERNEL // 2
D_FF = 4 * D_MODEL
GLU_COLS = 2 * CONV_CH
QKV_COLS = 3 * ATTN_WIDTH
RMS_EPS = 1e-6
LN_EPS = 1e-5
MASK_DIST = 1e30
BAND_HALF = 64
LANES = 128

V7X_VMEM_LIMIT_BYTES = 56 * 1024 * 1024

_F32 = jnp.float32
_BF16 = jnp.bfloat16

_SLOPES = np.asarray((2.0 ** (-8.0 / N_HEADS)) ** np.arange(1, N_HEADS + 1), dtype=np.float32)


def _params(semantics):
    return pltpu.CompilerParams(dimension_semantics=semantics, vmem_limit_bytes=V7X_VMEM_LIMIT_BYTES)


def _rmsnorm_kernel(x_ref, g_ref, o_ref):
    x = x_ref[...]
    ms = jnp.mean(x * x, axis=-1, keepdims=True)
    o_ref[...] = (x * lax.rsqrt(ms + RMS_EPS) * g_ref[...]).astype(o_ref.dtype)


def _rmsnorm(x, g, out_dtype, *, tm=256):
    m, dm = x.shape
    return pl.pallas_call(
        _rmsnorm_kernel,
        out_shape=jax.ShapeDtypeStruct((m, dm), out_dtype),
        grid=(m // tm,),
        in_specs=[pl.BlockSpec((tm, dm), lambda i: (i, 0)), pl.BlockSpec((1, dm), lambda i: (0, 0))],
        out_specs=pl.BlockSpec((tm, dm), lambda i: (i, 0)),
        compiler_params=_params(("parallel",)),
        name="rmsnorm",
    )(x, g.reshape(1, dm))


def _cast_kernel(w_ref, o_ref):
    o_ref[...] = w_ref[...].astype(o_ref.dtype)


def _cast_layer(w, layer, *, bk=512, bn=2048):
    _, k, n = w.shape
    return pl.pallas_call(
        _cast_kernel,
        out_shape=jax.ShapeDtypeStruct((k, n), _BF16),
        grid=(k // bk, n // bn),
        in_specs=[pl.BlockSpec((None, bk, bn), lambda i, j: (layer, i, j))],
        out_specs=pl.BlockSpec((bk, bn), lambda i, j: (i, j)),
        compiler_params=_params(("parallel", "parallel")),
        name="weight_cast",
    )(w)


def _mm_bias_kernel(a_ref, w_ref, b_ref, o_ref):
    acc = jnp.dot(a_ref[...], w_ref[...], preferred_element_type=_F32)
    o_ref[...] = (acc + b_ref[...]).astype(o_ref.dtype)


def _mm_relu2_kernel(a_ref, w_ref, o_ref):
    acc = jnp.dot(a_ref[...], w_ref[...], preferred_element_type=_F32)
    o_ref[...] = jnp.square(jnp.maximum(acc, 0.0)).astype(o_ref.dtype)


def _mm_heads_kernel(a_ref, w_ref, o_ref):
    acc = jnp.dot(a_ref[...], w_ref[...], preferred_element_type=_F32)
    for hd in range(o_ref.shape[0]):
        o_ref[hd] = acc[:, hd * HEAD_DIM:(hd + 1) * HEAD_DIM]


def _mm_two_res_kernel(c_ref, attn_ref, w_ref, r_ref, o_ref):
    kc = c_ref.shape[1]
    attn = jnp.concatenate([attn_ref[hd] for hd in range(attn_ref.shape[0])], axis=-1)
    acc = jnp.dot(c_ref[...], w_ref[:kc, :], preferred_element_type=_F32)
    acc += jnp.dot(attn, w_ref[kc:, :], preferred_element_type=_F32)
    o_ref[...] = acc + r_ref[...]


def _mm_ktiled_res_kernel(a_ref, w_ref, r_ref, o_ref):
    base = jnp.where(pl.program_id(2) == 0, r_ref[...], o_ref[...])
    o_ref[...] = base + jnp.dot(a_ref[...], w_ref[...], preferred_element_type=_F32)


def _matmul(kernel, a, w, extra=(), *, n_out, n_off=0, out_dtype, bm=1024, bn=1024, name):
    m, k = a.shape
    in_specs = [pl.BlockSpec((bm, k), lambda i, j: (i, 0)),
                pl.BlockSpec((k, bn), lambda i, j: (0, j + n_off // bn))]
    for e in extra:
        if e.shape[0] == 1:
            in_specs.append(pl.BlockSpec((1, bn), lambda i, j: (0, j)))
        else:
            in_specs.append(pl.BlockSpec((bm, bn), lambda i, j: (i, j)))
    return pl.pallas_call(
        kernel,
        out_shape=jax.ShapeDtypeStruct((m, n_out), out_dtype),
        grid=(m // bm, n_out // bn),
        in_specs=in_specs,
        out_specs=pl.BlockSpec((bm, bn), lambda i, j: (i, j)),
        compiler_params=_params(("parallel", "parallel")),
        name=name,
    )(a, w, *extra)


def _qkv_proj(a, w, *, n_off, bm=1024, bn=1024):
    m, k = a.shape
    hpb = bn // HEAD_DIM
    return pl.pallas_call(
        _mm_heads_kernel,
        out_shape=jax.ShapeDtypeStruct((QKV_COLS // HEAD_DIM, m, HEAD_DIM), _F32),
        grid=(m // bm, QKV_COLS // bn),
        in_specs=[pl.BlockSpec((bm, k), lambda i, j: (i, 0)),
                  pl.BlockSpec((k, bn), lambda i, j: (0, j + n_off // bn))],
        out_specs=pl.BlockSpec((hpb, bm, HEAD_DIM), lambda i, j: (j, i, 0)),
        compiler_params=_params(("parallel", "parallel")),
        name="in_proj_qkv",
    )(a, w)


def _out_proj(c, attn_hm, w, res, *, bm=1024, bn=1024):
    m, kc = c.shape
    nh, _, hd = attn_hm.shape
    n = w.shape[1]
    return pl.pallas_call(
        _mm_two_res_kernel,
        out_shape=jax.ShapeDtypeStruct((m, n), _F32),
        grid=(m // bm, n // bn),
        in_specs=[pl.BlockSpec((bm, kc), lambda i, j: (i, 0)),
                  pl.BlockSpec((nh, bm, hd), lambda i, j: (0, i, 0)),
                  pl.BlockSpec((kc + nh * hd, bn), lambda i, j: (0, j)),
                  pl.BlockSpec((bm, bn), lambda i, j: (i, j))],
        out_specs=pl.BlockSpec((bm, bn), lambda i, j: (i, j)),
        compiler_params=_params(("parallel", "parallel")),
        name="out_proj",
    )(c, attn_hm, w, res)


def _down_proj(a, w, res, *, bm=1024, bn=1024, bk=2048):
    m, k = a.shape
    n = w.shape[1]
    return pl.pallas_call(
        _mm_ktiled_res_kernel,
        out_shape=jax.ShapeDtypeStruct((m, n), _F32),
        grid=(m // bm, n // bn, k // bk),
        in_specs=[pl.BlockSpec((bm, bk), lambda i, j, kk: (i, kk)),
                  pl.BlockSpec((bk, bn), lambda i, j, kk: (kk, j)),
                  pl.BlockSpec((bm, bn), lambda i, j, kk: (i, j))],
        out_specs=pl.BlockSpec((bm, bn), lambda i, j, kk: (i, j)),
        compiler_params=_params(("parallel", "parallel", "arbitrary")),
        name="down_proj",
    )(a, w, res)


CONV_TQ = 512
CONV_HALO = 16
CONV_RB = 128
CONV_LN_ROWS = 16


def _conv_branch_kernel(ap_ref, ac_ref, an_ref, gp_ref, gc_ref, gn_ref, w_ref, b_ref, lg_ref, lb_ref,
                        o_ref, glu_ref, y_ref):
    i = pl.program_id(1)
    last = pl.num_programs(1) - 1
    tq = CONV_TQ
    nslab = CONV_CH // LANES

    def glu(a, g):
        return a * jax.nn.sigmoid(g)

    for sl in range(nslab):
        cs = slice(sl * LANES, (sl + 1) * LANES)
        glu_ref[sl, 0:CONV_HALO, :] = jnp.where(i > 0, glu(ap_ref[0, :, cs], gp_ref[0, :, cs]), 0.0)
        glu_ref[sl, CONV_HALO:CONV_HALO + tq, :] = glu(ac_ref[0, :, cs], gc_ref[0, :, cs])
        glu_ref[sl, CONV_HALO + tq:, :] = jnp.where(i < last, glu(an_ref[0, :, cs], gn_ref[0, :, cs]), 0.0)

    half_rows = CONV_RB // 2

    def block(rb, carry):
        r0 = pl.multiple_of(rb * CONV_RB, CONV_RB)
        for sl in range(nslab):
            cs = slice(sl * LANES, (sl + 1) * LANES)
            acc0 = jnp.broadcast_to(b_ref[:, cs], (half_rows, LANES))
            acc1 = acc0
            for t in range(CONV_KERNEL):
                w = w_ref[t:t + 1, cs]
                st = r0 + t + (CONV_HALO - CONV_HALF)
                acc0 = acc0 + glu_ref[sl, pl.ds(st, half_rows, stride=2), :] * w
                acc1 = acc1 + glu_ref[sl, pl.ds(st + 1, half_rows, stride=2), :] * w
            y_ref[sl, pl.ds(r0, half_rows, stride=2), :] = acc0
            y_ref[sl, pl.ds(r0 + 1, half_rows, stride=2), :] = acc1
        for c0 in range(0, CONV_RB, CONV_LN_ROWS):
            rows = pl.ds(r0 + c0, CONV_LN_ROWS)
            y = jnp.concatenate([y_ref[sl, rows, :] for sl in range(nslab)], axis=-1)
            mu = jnp.mean(y, axis=-1, keepdims=True)
            yc = y - mu
            var = jnp.mean(yc * yc, axis=-1, keepdims=True)
            z = yc * lax.rsqrt(var + LN_EPS) * lg_ref[...] + lb_ref[...]
            o_ref[0, rows, :] = (z * jax.nn.sigmoid(z)).astype(o_ref.dtype)
        return carry

    lax.fori_loop(0, tq // CONV_RB, block, 0)


def _conv_branch(glu_in, w_dw, b_dw, ln_g, ln_b):
    b, s, _ = glu_in.shape
    tq, halo = CONV_TQ, CONV_HALO
    nh = tq // halo
    last_halo = s // halo - 1
    c = CONV_CH

    def cur(col):
        return pl.BlockSpec((1, tq, c), lambda bi, i: (bi, i, col))

    def prev(col):
        return pl.BlockSpec((1, halo, c), lambda bi, i: (bi, jnp.maximum(i * nh - 1, 0), col))

    def nxt(col):
        return pl.BlockSpec((1, halo, c), lambda bi, i: (bi, jnp.minimum((i + 1) * nh, last_halo), col))

    def row(n):
        return pl.BlockSpec((n, c), lambda bi, i: (0, 0))

    return pl.pallas_call(
        _conv_branch_kernel,
        out_shape=jax.ShapeDtypeStruct((b, s, c), _BF16),
        grid=(b, s // tq),
        in_specs=[prev(0), cur(0), nxt(0), prev(1), cur(1), nxt(1), row(CONV_KERNEL), row(1), row(1), row(1)],
        out_specs=pl.BlockSpec((1, tq, c), lambda bi, i: (bi, i, 0)),
        scratch_shapes=[pltpu.VMEM((c // LANES, tq + 2 * halo, LANES), _F32),
                        pltpu.VMEM((c // LANES, tq, LANES), _F32)],
        compiler_params=_params(("parallel", "parallel")),
        name="conv_branch",
    )(glu_in, glu_in, glu_in, glu_in, glu_in, glu_in, w_dw, b_dw.reshape(1, c), ln_g.reshape(1, c),
      ln_b.reshape(1, c))


ATT_T = 1024
ATT_HB = 4
ATT_SB = 128


def _scores(q, k, slope_row, dist):
    s = lax.dot_general(q, k, (((1,), (1,)), ((), ())), preferred_element_type=_F32)
    return s * (HEAD_DIM ** -0.5) - slope_row * dist


def _softmax(s):
    m = jnp.max(s, axis=-1, keepdims=True)
    p = jnp.exp(s - m)
    den = jnp.sum(p, axis=-1, keepdims=True)
    return p.astype(_BF16), den, m + jnp.log(den)


def _attend_all(qs, ks, vs, slope_row, dists):
    scores = [_scores(q(), k(), slope_row, d()) for q, k, d in zip(qs, ks, dists)]
    probs = [_softmax(s) for s in scores]
    return [(jnp.dot(p, v(), preferred_element_type=_F32) / den, lse) for (p, den, lse), v in zip(probs, vs)]


def _attn_kernel(slopes_ref, qc_ref, kp_ref, kc_ref, kn_ref, vp_ref, vc_ref, vn_ref, out_ref,
                 q4_ref, k4_ref, v4_ref, o16s_ref, l16s_ref, o16_ref, l16_ref, o4_ref, l4_ref,
                 d1_ref, d4_ref, d16_ref):
    i = pl.program_id(1)
    head0 = pl.program_id(2) * ATT_HB
    is_first = i == 0
    is_last = i == pl.num_programs(1) - 1
    t, sb, half = ATT_T, ATT_SB, BAND_HALF
    kw = sb + 2 * half
    t4 = t // 4
    n16 = t // 16
    kw16 = n16 + 2 * half

    def dist_tile(nq, nk, dilation, mask_lo, mask_hi):
        row = lax.broadcasted_iota(jnp.int32, (nq, nk), 0)
        col = lax.broadcasted_iota(jnp.int32, (nq, nk), 1)
        rel = col - half - row
        valid = jnp.abs(rel) <= half
        if mask_lo is not None:
            valid = valid & ((col >= half) | jnp.logical_not(mask_lo))
        if mask_hi is not None:
            valid = valid & ((col < nk - half) | jnp.logical_not(mask_hi))
        return jnp.where(valid, (jnp.abs(rel) * dilation).astype(_F32), MASK_DIST)

    d1_ref[0] = dist_tile(sb, kw, 1, is_first, None)
    d1_ref[1] = dist_tile(sb, kw, 1, None, None)
    d1_ref[2] = dist_tile(sb, kw, 1, None, is_last)
    d4_ref[0] = dist_tile(sb, kw, 4, is_first, None)
    d4_ref[1] = dist_tile(sb, kw, 4, None, is_last)
    d16_ref[...] = dist_tile(n16, kw16, 16, is_first, is_last)

    def head(hh, carry):
        slope = slopes_ref[pl.ds(head0 + hh, 1), :]
        for r in range(4):
            q4_ref[r] = qc_ref.at[hh][pl.ds(r, t4, stride=4), :]
            for part, (ks, vs) in enumerate(((kp_ref, vp_ref), (kc_ref, vc_ref), (kn_ref, vn_ref))):
                k4_ref[r, part * t4:(part + 1) * t4, :] = ks.at[hh][pl.ds(r, t4, stride=4), :]
                v4_ref[r, part * t4:(part + 1) * t4, :] = vs.at[hh][pl.ds(r, t4, stride=4), :]

        rj = [(r, j) for r in range(4) for j in range(4)]
        res = _attend_all(
            [lambda r=r, j=j: q4_ref[r, pl.ds(j, n16, stride=4), :].astype(_BF16) for r, j in rj],
            [lambda r=r, j=j: k4_ref[r, pl.ds(j, kw16, stride=4), :].astype(_BF16) for r, j in rj],
            [lambda r=r, j=j: v4_ref[r, pl.ds(j, kw16, stride=4), :].astype(_BF16) for r, j in rj],
            slope[:, :kw16], [lambda: d16_ref[...]] * len(rj))
        for (r, j), (o, lse) in zip(rj, res):
            o16s_ref[r, pl.ds(j, n16, stride=4), :] = o
            l16s_ref[r, pl.ds(j, n16, stride=4), :] = jnp.broadcast_to(lse, (n16, HEAD_DIM))
        for r in range(4):
            o16_ref[pl.ds(r, t4, stride=4), :] = o16s_ref[r]
            l16_ref[pl.ds(r, t4, stride=4), :] = l16s_ref[r]

        rs = [(r, s_i) for r in range(4) for s_i in range(t4 // sb)]
        lo4 = lambda s_i: t4 + s_i * sb - half
        res = _attend_all(
            [lambda r=r, s_i=s_i: q4_ref[r, s_i * sb:(s_i + 1) * sb, :].astype(_BF16) for r, s_i in rs],
            [lambda r=r, s_i=s_i: k4_ref[r, lo4(s_i):lo4(s_i) + kw, :].astype(_BF16) for r, s_i in rs],
            [lambda r=r, s_i=s_i: v4_ref[r, lo4(s_i):lo4(s_i) + kw, :].astype(_BF16) for r, s_i in rs],
            slope, [lambda s_i=s_i: d4_ref[s_i] for _, s_i in rs])
        for (r, s_i), (o, lse) in zip(rs, res):
            o4_ref[pl.ds(4 * s_i * sb + r, sb, stride=4), :] = o
            l4_ref[pl.ds(4 * s_i * sb + r, sb, stride=4), :] = jnp.broadcast_to(lse, (sb, HEAD_DIM))

        nsub = t // sb

        def window(j, prev_ref, cur_ref, next_ref):
            lo = j * sb - half
            parts = []
            if lo < 0:
                parts.append(prev_ref[hh, t + lo:, :])
            parts.append(cur_ref[hh, max(lo, 0):min(lo + kw, t), :])
            if lo + kw > t:
                parts.append(next_ref[hh, :lo + kw - t, :])
            x = parts[0] if len(parts) == 1 else jnp.concatenate(parts, axis=0)
            return x.astype(_BF16)

        res = _attend_all(
            [lambda j=j: qc_ref[hh, j * sb:(j + 1) * sb, :].astype(_BF16) for j in range(nsub)],
            [lambda j=j: window(j, kp_ref, kc_ref, kn_ref) for j in range(nsub)],
            [lambda j=j: window(j, vp_ref, vc_ref, vn_ref) for j in range(nsub)],
            slope, [lambda j=j: d1_ref[0 if j == 0 else (2 if j == nsub - 1 else 1)] for j in range(nsub)])
        for j, (o1, lse1) in enumerate(res):
            rows = slice(j * sb, (j + 1) * sb)
            l1 = jnp.broadcast_to(lse1, (sb, HEAD_DIM))
            l4, l16 = l4_ref[rows, :], l16_ref[rows, :]
            mx = jnp.maximum(jnp.maximum(l1, l4), l16)
            e1, e4, e16 = jnp.exp(l1 - mx), jnp.exp(l4 - mx), jnp.exp(l16 - mx)
            num = e1 * o1 + e4 * o4_ref[rows, :] + e16 * o16_ref[rows, :]
            out_ref[hh, rows, :] = (num / (e1 + e4 + e16)).astype(out_ref.dtype)
        return carry

    lax.fori_loop(0, ATT_HB, head, 0)


def _attention(qkv_hm, batch, seq):
    t, hb = ATT_T, ATT_HB
    nt = seq // t
    ng = N_HEADS // hb
    m = batch * seq
    slopes = jnp.asarray(np.repeat(_SLOPES[:, None], 2 * ATT_SB, axis=1))

    def spec(part, shift):
        def index(b, i, g):
            return (part * ng + g, b * nt + jnp.clip(i + shift, 0, nt - 1), 0)
        return pl.BlockSpec((hb, t, HEAD_DIM), index)

    t4, n16 = t // 4, t // 16

    def f32(*shape):
        return pltpu.VMEM(shape, _F32)

    return pl.pallas_call(
        _attn_kernel,
        out_shape=jax.ShapeDtypeStruct((N_HEADS, m, HEAD_DIM), _BF16),
        grid=(batch, nt, ng),
        in_specs=[pl.BlockSpec((N_HEADS, 2 * ATT_SB), lambda b, i, g: (0, 0)),
                  spec(0, 0), spec(1, -1), spec(1, 0), spec(1, 1), spec(2, -1), spec(2, 0), spec(2, 1)],
        out_specs=pl.BlockSpec((hb, t, HEAD_DIM), lambda b, i, g: (g, b * nt + i, 0)),
        scratch_shapes=[f32(4, t4, HEAD_DIM), f32(4, 3 * t4, HEAD_DIM), f32(4, 3 * t4, HEAD_DIM),
                        f32(4, t4, HEAD_DIM), f32(4, t4, HEAD_DIM),
                        f32(t, HEAD_DIM), f32(t, HEAD_DIM), f32(t, HEAD_DIM), f32(t, HEAD_DIM),
                        f32(3, ATT_SB, 2 * ATT_SB), f32(2, ATT_SB, 2 * ATT_SB), f32(n16, n16 + 2 * BAND_HALF)],
        compiler_params=_params(("parallel", "parallel", "parallel")),
        name="dilated_attention",
    )(slopes, qkv_hm, qkv_hm, qkv_hm, qkv_hm, qkv_hm, qkv_hm, qkv_hm)


def kernel(x, norm1_g, w_in, b_glu, w_dw, b_dw, ln_g, ln_b, w_out, norm2_g, w_up, w_down, final_g):
    b, s, dm = x.shape
    m = b * s
    depth = w_in.shape[0]
    x = x.reshape(m, dm)
    for l in range(depth):
        w_in_l = _cast_layer(w_in, l)
        w_out_l = _cast_layer(w_out, l)
        w_up_l = _cast_layer(w_up, l)
        w_down_l = _cast_layer(w_down, l)

        h = _rmsnorm(x, norm1_g[l], _BF16)
        glu_in = _matmul(_mm_bias_kernel, h, w_in_l, (b_glu[l].reshape(1, GLU_COLS),), n_out=GLU_COLS,
                         out_dtype=_F32, name="in_proj_glu")
        qkv_hm = _qkv_proj(h, w_in_l, n_off=GLU_COLS)

        c = _conv_branch(glu_in.reshape(b, s, GLU_COLS), w_dw[l], b_dw[l], ln_g[l], ln_b[l])
        attn_hm = _attention(qkv_hm, b, s)

        x = _out_proj(c.reshape(m, CONV_CH), attn_hm, w_out_l, x)
        h2 = _rmsnorm(x, norm2_g[l], _BF16)
        hid = _matmul(_mm_relu2_kernel, h2, w_up_l, n_out=D_FF, out_dtype=_BF16, name="mlp_up")
        x = _down_proj(hid, w_down_l, x)
    return _rmsnorm(x, final_g, _F32).reshape(b, s, dm)
```

```python
import jax
import jax.numpy as jnp
import numpy as np
from jax import lax
from jax.experimental import pallas as pl
from jax.experimental.pallas import tpu as pltpu

D_MODEL = 4096
HEAD_DIM = 128
CONV_CH = 2048
ATTN_WIDTH = 2048
N_HEADS = ATTN_WIDTH // HEAD_DIM
CONV_KERNEL = 31
CONV_HALF = CONV_KERNEL // 2
D_FF = 4 * D_MODEL
GLU_COLS = 2 * CONV_CH
QKV_COLS = 3 * ATTN_WIDTH
RMS_EPS = 1e-6
LN_EPS = 1e-5
MASK_DIST = 1e30
BAND_HALF = 64
LANES = 128

V7X_VMEM_LIMIT_BYTES = 56 * 1024 * 1024

_F32 = jnp.float32
_BF16 = jnp.bfloat16

_SLOPES = np.asarray((2.0 ** (-8.0 / N_HEADS)) ** np.arange(1, N_HEADS + 1), dtype=np.float32)


def _params(semantics):
    return pltpu.CompilerParams(dimension_semantics=semantics, vmem_limit_bytes=V7X_VMEM_LIMIT_BYTES)


def _rmsnorm_kernel(x_ref, g_ref, o_ref):
    x = x_ref[...]
    ms = jnp.mean(x * x, axis=-1, keepdims=True)
    o_ref[...] = (x * lax.rsqrt(ms + RMS_EPS) * g_ref[...]).astype(o_ref.dtype)


def _rmsnorm(x, g, out_dtype, *, tm=256):
    m, dm = x.shape
    return pl.pallas_call(
        _rmsnorm_kernel,
        out_shape=jax.ShapeDtypeStruct((m, dm), out_dtype),
        grid=(m // tm,),
        in_specs=[pl.BlockSpec((tm, dm), lambda i: (i, 0)), pl.BlockSpec((1, dm), lambda i: (0, 0))],
        out_specs=pl.BlockSpec((tm, dm), lambda i: (i, 0)),
        compiler_params=_params(("parallel",)),
        name="rmsnorm",
    )(x, g.reshape(1, dm))


def _cast_kernel(w_ref, o_ref):
    o_ref[...] = w_ref[...].astype(o_ref.dtype)


def _cast_layer(w, layer, *, bk=512, bn=2048):
    _, k, n = w.shape
    return pl.pallas_call(
        _cast_kernel,
        out_shape=jax.ShapeDtypeStruct((k, n), _BF16),
        grid=(k // bk, n // bn),
        in_specs=[pl.BlockSpec((None, bk, bn), lambda i, j: (layer, i, j))],
        out_specs=pl.BlockSpec((bk, bn), lambda i, j: (i, j)),
        compiler_params=_params(("parallel", "parallel")),
        name="weight_cast",
    )(w)


SSQ_BLOCKS = 4
SSQ_COLS = SSQ_BLOCKS * LANES


def _emit_norm_operand(x, g_ref, xg_ref, ssq_ref):
    xg_ref[...] = (x * g_ref[...]).astype(xg_ref.dtype)
    ssq_ref[...] = jnp.broadcast_to(jnp.sum(x * x, axis=-1, keepdims=True), ssq_ref.shape)


def _row_scale(ssq_ref, width):
    tot = ssq_ref[:, 0:LANES]
    for blk in range(1, SSQ_BLOCKS):
        tot = tot + ssq_ref[:, blk * LANES:(blk + 1) * LANES]
    scale = lax.rsqrt(tot * (1.0 / D_MODEL) + RMS_EPS)
    return jnp.concatenate([scale] * (width // LANES), axis=1)


def _cast_block(wsrc_ref, wdst_ref):
    wdst_ref[...] = wsrc_ref[...].astype(wdst_ref.dtype)


def _mm_bias_kernel(a_ref, ssq_ref, w_ref, b_ref, o_ref):
    acc = jnp.dot(a_ref[...], w_ref[...], preferred_element_type=_F32)
    o_ref[...] = acc * _row_scale(ssq_ref, acc.shape[1]) + b_ref[...]


def _mm_relu2_kernel(a_ref, ssq_ref, w_ref, wsrc_ref, o_ref, wdst_ref):
    acc = jnp.dot(a_ref[...], w_ref[...], preferred_element_type=_F32)
    acc = acc * _row_scale(ssq_ref, acc.shape[1])
    o_ref[...] = jnp.square(jnp.maximum(acc, 0.0)).astype(o_ref.dtype)
    _cast_block(wsrc_ref, wdst_ref)


def _mm_heads_kernel(a_ref, ssq_ref, w_ref, o_ref):
    acc = jnp.dot(a_ref[...], w_ref[...], preferred_element_type=_F32)
    scale = _row_scale(ssq_ref, HEAD_DIM)
    for hd in range(o_ref.shape[0]):
        o_ref[hd] = acc[:, hd * HEAD_DIM:(hd + 1) * HEAD_DIM] * scale


def _mm_two_res_kernel(c_ref, attn_ref, w_ref, r_ref, g_ref, o_ref, xg_ref, ssq_ref):
    kc = c_ref.shape[1]
    attn = jnp.concatenate([attn_ref[hd] for hd in range(attn_ref.shape[0])], axis=-1)
    acc = jnp.dot(c_ref[...], w_ref[:kc, :], preferred_element_type=_F32)
    acc += jnp.dot(attn, w_ref[kc:, :], preferred_element_type=_F32)
    x = acc + r_ref[...]
    o_ref[...] = x
    _emit_norm_operand(x, g_ref, xg_ref, ssq_ref)


def _ktiled_sum(a_ref, w_ref, r_ref, o_ref):
    base = jnp.where(pl.program_id(2) == 0, r_ref[...], o_ref[...])
    x = base + jnp.dot(a_ref[...], w_ref[...], preferred_element_type=_F32)
    o_ref[...] = x
    return x


def _mm_ktiled_res_kernel(a_ref, w_ref, r_ref, o_ref):
    _ktiled_sum(a_ref, w_ref, r_ref, o_ref)


def _mm_ktiled_res_next_kernel(a_ref, w_ref, r_ref, g_ref, wsrc0_ref, wsrc1_ref,
                               o_ref, xg_ref, ssq_ref, wdst0_ref, wdst1_ref):
    _ktiled_sum(a_ref, w_ref, r_ref, o_ref)
    _cast_block(wsrc0_ref, wdst0_ref)
    _cast_block(wsrc1_ref, wdst1_ref)

    @pl.when(pl.program_id(2) == pl.num_programs(2) - 1)
    def _():
        _emit_norm_operand(o_ref[...], g_ref, xg_ref, ssq_ref)


MM_BM = 1024
MM_BN = 1024
MM_BK = 2048
CAST_ROWS = 16


def _hosted_cast(w_stack, layer, rows, block_of_step):
    _, k, n = w_stack.shape
    src = pl.BlockSpec((None, rows, n), lambda *ids: (layer, block_of_step(*ids), 0))
    dst = pl.BlockSpec((rows, n), lambda *ids: (block_of_step(*ids), 0))
    return src, dst, jax.ShapeDtypeStruct((k, n), _BF16)


def _norm_prep_kernel(x_ref, g_ref, xg_ref, ssq_ref):
    x = x_ref[...]
    xg_ref[...] = (x * g_ref[...]).astype(xg_ref.dtype)
    total = jnp.broadcast_to(jnp.sum(x * x, axis=-1, keepdims=True), (x.shape[0], LANES))
    ssq_ref[...] = jnp.concatenate([total] + [jnp.zeros_like(total)] * (SSQ_BLOCKS - 1), axis=1)


def _norm_prep(x, g, *, tm=256):
    m, dm = x.shape
    return pl.pallas_call(
        _norm_prep_kernel,
        out_shape=[jax.ShapeDtypeStruct((m, dm), _BF16), jax.ShapeDtypeStruct((m, SSQ_COLS), _F32)],
        grid=(m // tm,),
        in_specs=[pl.BlockSpec((tm, dm), lambda i: (i, 0)), pl.BlockSpec((1, dm), lambda i: (0, 0))],
        out_specs=[pl.BlockSpec((tm, dm), lambda i: (i, 0)), pl.BlockSpec((tm, SSQ_COLS), lambda i: (i, 0))],
        compiler_params=_params(("parallel",)),
        name="norm_prep",
    )(x, g.reshape(1, dm))


def _a_specs(k):
    return [pl.BlockSpec((MM_BM, k), lambda i, j: (i, 0)), pl.BlockSpec((MM_BM, SSQ_COLS), lambda i, j: (i, 0))]


def _in_proj_glu(xg, ssq, w, bias):
    m, k = xg.shape
    return pl.pallas_call(
        _mm_bias_kernel,
        out_shape=jax.ShapeDtypeStruct((m, GLU_COLS), _F32),
        grid=(m // MM_BM, GLU_COLS // MM_BN),
        in_specs=_a_specs(k) + [pl.BlockSpec((k, MM_BN), lambda i, j: (0, j)),
                                pl.BlockSpec((1, MM_BN), lambda i, j: (0, j))],
        out_specs=pl.BlockSpec((MM_BM, MM_BN), lambda i, j: (i, j)),
        compiler_params=_params(("parallel", "parallel")),
        name="in_proj_glu",
    )(xg, ssq, w, bias.reshape(1, GLU_COLS))


def _qkv_proj(xg, ssq, w):
    m, k = xg.shape
    hpb = MM_BN // HEAD_DIM
    return pl.pallas_call(
        _mm_heads_kernel,
        out_shape=jax.ShapeDtypeStruct((QKV_COLS // HEAD_DIM, m, HEAD_DIM), _F32),
        grid=(m // MM_BM, QKV_COLS // MM_BN),
        in_specs=_a_specs(k) + [pl.BlockSpec((k, MM_BN), lambda i, j: (0, j + GLU_COLS // MM_BN))],
        out_specs=pl.BlockSpec((hpb, MM_BM, HEAD_DIM), lambda i, j: (j, i, 0)),
        compiler_params=_params(("parallel", "parallel")),
        name="in_proj_qkv",
    )(xg, ssq, w)


def _mlp_up(xg, ssq, w, cast_w, cast_layer):
    m, k = xg.shape
    gi, gj = m // MM_BM, D_FF // MM_BN
    src, dst, dst_shape = _hosted_cast(cast_w, cast_layer, cast_w.shape[1] // (gi * gj), lambda i, j: i * gj + j)
    return pl.pallas_call(
        _mm_relu2_kernel,
        out_shape=[jax.ShapeDtypeStruct((m, D_FF), _BF16), dst_shape],
        grid=(gi, gj),
        in_specs=_a_specs(k) + [pl.BlockSpec((k, MM_BN), lambda i, j: (0, j)), src],
        out_specs=[pl.BlockSpec((MM_BM, MM_BN), lambda i, j: (i, j)), dst],
        compiler_params=_params(("parallel", "parallel")),
        name="mlp_up",
    )(xg, ssq, w, cast_w)


def _out_proj(c, attn_hm, w, res, g_next):
    m, kc = c.shape
    nh, _, hd = attn_hm.shape
    n = w.shape[1]
    tile = pl.BlockSpec((MM_BM, MM_BN), lambda j, i: (i, j))
    return pl.pallas_call(
        _mm_two_res_kernel,
        out_shape=[jax.ShapeDtypeStruct((m, n), _F32), jax.ShapeDtypeStruct((m, n), _BF16),
                   jax.ShapeDtypeStruct((m, SSQ_COLS), _F32)],
        grid=(n // MM_BN, m // MM_BM),
        in_specs=[pl.BlockSpec((MM_BM, kc), lambda j, i: (i, 0)),
                  pl.BlockSpec((nh, MM_BM, hd), lambda j, i: (0, i, 0)),
                  pl.BlockSpec((kc + nh * hd, MM_BN), lambda j, i: (0, j), pipeline_mode=pl.Buffered(1)),
                  tile,
                  pl.BlockSpec((1, MM_BN), lambda j, i: (0, j))],
        out_specs=[tile, tile, pl.BlockSpec((MM_BM, LANES), lambda j, i: (i, j))],
        compiler_params=_params(("parallel", "parallel")),
        name="out_proj",
    )(c, attn_hm, w, res, g_next.reshape(1, n))


def _down_proj_specs(k, n):
    tile = pl.BlockSpec((MM_BM, MM_BN), lambda i, j, kk: (i, j))
    return [pl.BlockSpec((MM_BM, MM_BK), lambda i, j, kk: (i, kk)),
            pl.BlockSpec((MM_BK, MM_BN), lambda i, j, kk: (kk, j)),
            tile], tile


def _down_proj_last(a, w, res):
    m, k = a.shape
    n = w.shape[1]
    in_specs, tile = _down_proj_specs(k, n)
    return pl.pallas_call(
        _mm_ktiled_res_kernel,
        out_shape=jax.ShapeDtypeStruct((m, n), _F32),
        grid=(m // MM_BM, n // MM_BN, k // MM_BK),
        in_specs=in_specs,
        out_specs=tile,
        compiler_params=_params(("parallel", "parallel", "arbitrary")),
        name="down_proj_last",
    )(a, w, res)


def _down_proj(a, w, res, g_next, cast_w0, cast_w1, cast_layer):
    m, k = a.shape
    n = w.shape[1]
    gi, gj, gk = m // MM_BM, n // MM_BN, k // MM_BK
    in_specs, tile = _down_proj_specs(k, n)
    casts = []
    steps = gi * gj * gk
    for cw in (cast_w0, cast_w1):
        rows = max(CAST_ROWS, cw.shape[1] // steps)
        steps_per_block = steps // (cw.shape[1] // rows)
        casts.append(_hosted_cast(cw, cast_layer, rows,
                                  lambda i, j, kk, spb=steps_per_block: ((i * gj + j) * gk + kk) // spb))
    return pl.pallas_call(
        _mm_ktiled_res_next_kernel,
        out_shape=[jax.ShapeDtypeStruct((m, n), _F32), jax.ShapeDtypeStruct((m, n), _BF16),
                   jax.ShapeDtypeStruct((m, SSQ_COLS), _F32), casts[0][2], casts[1][2]],
        grid=(gi, gj, gk),
        in_specs=in_specs + [pl.BlockSpec((1, MM_BN), lambda i, j, kk: (0, j)), casts[0][0], casts[1][0]],
        out_specs=[tile, tile, pl.BlockSpec((MM_BM, LANES), lambda i, j, kk: (i, j)), casts[0][1], casts[1][1]],
        compiler_params=_params(("parallel", "parallel", "arbitrary")),
        name="down_proj",
    )(a, w, res, g_next.reshape(1, n), cast_w0, cast_w1)


CONV_TQ = 512
CONV_HALO = 16
CONV_RB = 128
CONV_LN_ROWS = 16


def _conv_branch_kernel(ap_ref, ac_ref, an_ref, gp_ref, gc_ref, gn_ref, w_ref, b_ref, lg_ref, lb_ref,
                        o_ref, glu_ref, y_ref):
    i = pl.program_id(1)
    last = pl.num_programs(1) - 1
    tq = CONV_TQ
    nslab = CONV_CH // LANES

    def glu(a, g):
        return a * jax.nn.sigmoid(g)

    for sl in range(nslab):
        cs = slice(sl * LANES, (sl + 1) * LANES)
        glu_ref[sl, 0:CONV_HALO, :] = jnp.where(i > 0, glu(ap_ref[0, :, cs], gp_ref[0, :, cs]), 0.0)
        glu_ref[sl, CONV_HALO:CONV_HALO + tq, :] = glu(ac_ref[0, :, cs], gc_ref[0, :, cs])
        glu_ref[sl, CONV_HALO + tq:, :] = jnp.where(i < last, glu(an_ref[0, :, cs], gn_ref[0, :, cs]), 0.0)

    half_rows = CONV_RB // 2

    def block(rb, carry):
        r0 = pl.multiple_of(rb * CONV_RB, CONV_RB)
        for sl in range(nslab):
            cs = slice(sl * LANES, (sl + 1) * LANES)
            acc0 = jnp.broadcast_to(b_ref[:, cs], (half_rows, LANES))
            acc1 = acc0
            for t in range(CONV_KERNEL):
                w = w_ref[t:t + 1, cs]
                st = r0 + t + (CONV_HALO - CONV_HALF)
                acc0 = acc0 + glu_ref[sl, pl.ds(st, half_rows, stride=2), :] * w
                acc1 = acc1 + glu_ref[sl, pl.ds(st + 1, half_rows, stride=2), :] * w
            y_ref[sl, pl.ds(r0, half_rows, stride=2), :] = acc0
            y_ref[sl, pl.ds(r0 + 1, half_rows, stride=2), :] = acc1
        for c0 in range(0, CONV_RB, CONV_LN_ROWS):
            rows = pl.ds(r0 + c0, CONV_LN_ROWS)
            y = jnp.concatenate([y_ref[sl, rows, :] for sl in range(nslab)], axis=-1)
            mu = jnp.mean(y, axis=-1, keepdims=True)
            yc = y - mu
            var = jnp.mean(yc * yc, axis=-1, keepdims=True)
            z = yc * lax.rsqrt(var + LN_EPS) * lg_ref[...] + lb_ref[...]
            o_ref[0, rows, :] = (z * jax.nn.sigmoid(z)).astype(o_ref.dtype)
        return carry

    lax.fori_loop(0, tq // CONV_RB, block, 0)


def _conv_branch(glu_in, w_dw, b_dw, ln_g, ln_b):
    b, s, _ = glu_in.shape
    tq, halo = CONV_TQ, CONV_HALO
    nh = tq // halo
    last_halo = s // halo - 1
    c = CONV_CH

    def cur(col):
        return pl.BlockSpec((1, tq, c), lambda bi, i: (bi, i, col))

    def prev(col):
        return pl.BlockSpec((1, halo, c), lambda bi, i: (bi, jnp.maximum(i * nh - 1, 0), col))

    def nxt(col):
        return pl.BlockSpec((1, halo, c), lambda bi, i: (bi, jnp.minimum((i + 1) * nh, last_halo), col))

    def row(n):
        return pl.BlockSpec((n, c), lambda bi, i: (0, 0))

    return pl.pallas_call(
        _conv_branch_kernel,
        out_shape=jax.ShapeDtypeStruct((b, s, c), _BF16),
        grid=(b, s // tq),
        in_specs=[prev(0), cur(0), nxt(0), prev(1), cur(1), nxt(1), row(CONV_KERNEL), row(1), row(1), row(1)],
        out_specs=pl.BlockSpec((1, tq, c), lambda bi, i: (bi, i, 0)),
        scratch_shapes=[pltpu.VMEM((c // LANES, tq + 2 * halo, LANES), _F32),
                        pltpu.VMEM((c // LANES, tq, LANES), _F32)],
        compiler_params=_params(("parallel", "parallel")),
        name="conv_branch",
    )(glu_in, glu_in, glu_in, glu_in, glu_in, glu_in, w_dw, b_dw.reshape(1, c), ln_g.reshape(1, c),
      ln_b.reshape(1, c))


ATT_T = 1024
ATT_HB = 4
ATT_SB = 128


def _scores(q, k, slope_row, dist):
    s = lax.dot_general(q, k, (((1,), (1,)), ((), ())), preferred_element_type=_F32)
    return s * (HEAD_DIM ** -0.5) - slope_row * dist


def _softmax(s):
    m = jnp.max(s, axis=-1, keepdims=True)
    p = jnp.exp(s - m)
    den = jnp.sum(p, axis=-1, keepdims=True)
    return p.astype(_BF16), den, m + jnp.log(den)


def _attend_all(qs, ks, vs, slope_row, dists):
    scores = [_scores(q(), k(), slope_row, d()) for q, k, d in zip(qs, ks, dists)]
    probs = [_softmax(s) for s in scores]
    return [(jnp.dot(p, v(), preferred_element_type=_F32) / den, lse) for (p, den, lse), v in zip(probs, vs)]


def _attn_kernel(slopes_ref, qc_ref, kp_ref, kc_ref, kn_ref, vp_ref, vc_ref, vn_ref, out_ref,
                 q4_ref, k4_ref, v4_ref, o16s_ref, l16s_ref, o16_ref, l16_ref, o4_ref, l4_ref,
                 d1_ref, d4_ref, d16_ref):
    i = pl.program_id(1)
    head0 = pl.program_id(2) * ATT_HB
    is_first = i == 0
    is_last = i == pl.num_programs(1) - 1
    t, sb, half = ATT_T, ATT_SB, BAND_HALF
    kw = sb + 2 * half
    t4 = t // 4
    n16 = t // 16
    kw16 = n16 + 2 * half

    def dist_tile(nq, nk, dilation, mask_lo, mask_hi):
        row = lax.broadcasted_iota(jnp.int32, (nq, nk), 0)
        col = lax.broadcasted_iota(jnp.int32, (nq, nk), 1)
        rel = col - half - row
        valid = jnp.abs(rel) <= half
        if mask_lo is not None:
            valid = valid & ((col >= half) | jnp.logical_not(mask_lo))
        if mask_hi is not None:
            valid = valid & ((col < nk - half) | jnp.logical_not(mask_hi))
        return jnp.where(valid, (jnp.abs(rel) * dilation).astype(_F32), MASK_DIST)

    d1_ref[0] = dist_tile(sb, kw, 1, is_first, None)
    d1_ref[1] = dist_tile(sb, kw, 1, None, None)
    d1_ref[2] = dist_tile(sb, kw, 1, None, is_last)
    d4_ref[0] = dist_tile(sb, kw, 4, is_first, None)
    d4_ref[1] = dist_tile(sb, kw, 4, None, is_last)
    d16_ref[...] = dist_tile(n16, kw16, 16, is_first, is_last)

    def head(hh, carry):
        slope = slopes_ref[pl.ds(head0 + hh, 1), :]
        for r in range(4):
            q4_ref[r] = qc_ref.at[hh][pl.ds(r, t4, stride=4), :]
            for part, (ks, vs) in enumerate(((kp_ref, vp_ref), (kc_ref, vc_ref), (kn_ref, vn_ref))):
                k4_ref[r, part * t4:(part + 1) * t4, :] = ks.at[hh][pl.ds(r, t4, stride=4), :]
                v4_ref[r, part * t4:(part + 1) * t4, :] = vs.at[hh][pl.ds(r, t4, stride=4), :]

        rj = [(r, j) for r in range(4) for j in range(4)]
        res = _attend_all(
            [lambda r=r, j=j: q4_ref[r, pl.ds(j, n16, stride=4), :].astype(_BF16) for r, j in rj],
            [lambda r=r, j=j: k4_ref[r, pl.ds(j, kw16, stride=4), :].astype(_BF16) for r, j in rj],
            [lambda r=r, j=j: v4_ref[r, pl.ds(j, kw16, stride=4), :].astype(_BF16) for r, j in rj],
            slope[:, :kw16], [lambda: d16_ref[...]] * len(rj))
        for (r, j), (o, lse) in zip(rj, res):
            o16s_ref[r, pl.ds(j, n16, stride=4), :] = o
            l16s_ref[r, pl.ds(j, n16, stride=4), :] = jnp.broadcast_to(lse, (n16, HEAD_DIM))
        for r in range(4):
            o16_ref[pl.ds(r, t4, stride=4), :] = o16s_ref[r]
            l16_ref[pl.ds(r, t4, stride=4), :] = l16s_ref[r]

        rs = [(r, s_i) for r in range(4) for s_i in range(t4 // sb)]
        lo4 = lambda s_i: t4 + s_i * sb - half
        res = _attend_all(
            [lambda r=r, s_i=s_i: q4_ref[r, s_i * sb:(s_i + 1) * sb, :].astype(_BF16) for r, s_i in rs],
            [lambda r=r, s_i=s_i: k4_ref[r, lo4(s_i):lo4(s_i) + kw, :].astype(_BF16) for r, s_i in rs],
            [lambda r=r, s_i=s_i: v4_ref[r, lo4(s_i):lo4(s_i) + kw, :].astype(_BF16) for r, s_i in rs],
            slope, [lambda s_i=s_i: d4_ref[s_i] for _, s_i in rs])
        for (r, s_i), (o, lse) in zip(rs, res):
            o4_ref[pl.ds(4 * s_i * sb + r, sb, stride=4), :] = o
            l4_ref[pl.ds(4 * s_i * sb + r, sb, stride=4), :] = jnp.broadcast_to(lse, (sb, HEAD_DIM))

        nsub = t // sb

        def window(j, prev_ref, cur_ref, next_ref):
            lo = j * sb - half
            parts = []
            if lo < 0:
                parts.append(prev_ref[hh, t + lo:, :])
            parts.append(cur_ref[hh, max(lo, 0):min(lo + kw, t), :])
            if lo + kw > t:
                parts.append(next_ref[hh, :lo + kw - t, :])
            x = parts[0] if len(parts) == 1 else jnp.concatenate(parts, axis=0)
            return x.astype(_BF16)

        res = _attend_all(
            [lambda j=j: qc_ref[hh, j * sb:(j + 1) * sb, :].astype(_BF16) for j in range(nsub)],
            [lambda j=j: window(j, kp_ref, kc_ref, kn_ref) for j in range(nsub)],
            [lambda j=j: window(j, vp_ref, vc_ref, vn_ref) for j in range(nsub)],
            slope, [lambda j=j: d1_ref[0 if j == 0 else (2 if j == nsub - 1 else 1)] for j in range(nsub)])
        for j, (o1, lse1) in enumerate(res):
            rows = slice(j * sb, (j + 1) * sb)
            l1 = jnp.broadcast_to(lse1, (sb, HEAD_DIM))
            l4, l16 = l4_ref[rows, :], l16_ref[rows, :]
            mx = jnp.maximum(jnp.maximum(l1, l4), l16)
            e1, e4, e16 = jnp.exp(l1 - mx), jnp.exp(l4 - mx), jnp.exp(l16 - mx)
            num = e1 * o1 + e4 * o4_ref[rows, :] + e16 * o16_ref[rows, :]
            out_ref[hh, rows, :] = (num / (e1 + e4 + e16)).astype(out_ref.dtype)
        return carry

    lax.fori_loop(0, ATT_HB, head, 0)


def _attention(qkv_hm, batch, seq):
    t, hb = ATT_T, ATT_HB
    nt = seq // t
    ng = N_HEADS // hb
    m = batch * seq
    slopes = jnp.asarray(np.repeat(_SLOPES[:, None], 2 * ATT_SB, axis=1))

    def spec(part, shift):
        def index(b, i, g):
            return (part * ng + g, b * nt + jnp.clip(i + shift, 0, nt - 1), 0)
        return pl.BlockSpec((hb, t, HEAD_DIM), index)

    t4, n16 = t // 4, t // 16

    def f32(*shape):
        return pltpu.VMEM(shape, _F32)

    return pl.pallas_call(
        _attn_kernel,
        out_shape=jax.ShapeDtypeStruct((N_HEADS, m, HEAD_DIM), _BF16),
        grid=(batch, nt, ng),
        in_specs=[pl.BlockSpec((N_HEADS, 2 * ATT_SB), lambda b, i, g: (0, 0)),
                  spec(0, 0), spec(1, -1), spec(1, 0), spec(1, 1), spec(2, -1), spec(2, 0), spec(2, 1)],
        out_specs=pl.BlockSpec((hb, t, HEAD_DIM), lambda b, i, g: (g, b * nt + i, 0)),
        scratch_shapes=[f32(4, t4, HEAD_DIM), f32(4, 3 * t4, HEAD_DIM), f32(4, 3 * t4, HEAD_DIM),
                        f32(4, t4, HEAD_DIM), f32(4, t4, HEAD_DIM),
                        f32(t, HEAD_DIM), f32(t, HEAD_DIM), f32(t, HEAD_DIM), f32(t, HEAD_DIM),
                        f32(3, ATT_SB, 2 * ATT_SB), f32(2, ATT_SB, 2 * ATT_SB), f32(n16, n16 + 2 * BAND_HALF)],
        compiler_params=_params(("parallel", "parallel", "parallel")),
        name="dilated_attention",
    )(slopes, qkv_hm, qkv_hm, qkv_hm, qkv_hm, qkv_hm, qkv_hm, qkv_hm)


def kernel(x, norm1_g, w_in, b_glu, w_dw, b_dw, ln_g, ln_b, w_out, norm2_g, w_up, w_down, final_g):
    b, s, dm = x.shape
    m = b * s
    depth = w_in.shape[0]
    x = x.reshape(m, dm)
    w_in_l = _cast_layer(w_in, 0)
    w_up_l = _cast_layer(w_up, 0)
    xg, ssq = _norm_prep(x, norm1_g[0])
    for l in range(depth):
        w_out_l = _cast_layer(w_out, l)
        glu_in = _in_proj_glu(xg, ssq, w_in_l, b_glu[l])
        qkv_hm = _qkv_proj(xg, ssq, w_in_l)

        c = _conv_branch(glu_in.reshape(b, s, GLU_COLS), w_dw[l], b_dw[l], ln_g[l], ln_b[l])
        attn_hm = _attention(qkv_hm, b, s)

        x, xg, ssq = _out_proj(c.reshape(m, CONV_CH), attn_hm, w_out_l, x, norm2_g[l])
        hid, w_down_l = _mlp_up(xg, ssq, w_up_l, w_down, l)
        if l + 1 < depth:
            x, xg, ssq, w_in_l, w_up_l = _down_proj(hid, w_down_l, x, norm1_g[l + 1], w_in, w_up, l + 1)
        else:
            x = _down_proj_last(hid, w_down_l, x)
    return _rmsnorm(x, final_g, _F32).reshape(b, s, dm)
```
